```python
import jax, jax.numpy as jnp
from jax import lax
import numpy as np

D_MODEL = 1024
BATCH = 4
SEQ = 8192
DEPTH = 2
DEC_BATCH = 128
DEC_SEQ = 8
PAST_LEN = 16384
PAGE_SIZE = 128

N_META = 16
N_A_LAYERS = DEPTH // 2
N_B_LAYERS = DEPTH - N_A_LAYERS
CONV_W = 3
D_FF = 4 * D_MODEL
N_HEADS = 16
QK_NOPE = 64
QK_ROPE = 32
V_DIM = 64
KV_RANK = D_MODEL // 4
Q_RANK = 3 * KV_RANK
ROPE_THETA = 10000.0
Q_BLOCK = 128
LN_EPS = 1e-5
RMS_EPS = 1e-6
DN_ALPHA = (2.0 * DEPTH) ** 0.25
DN_BETA = (8.0 * DEPTH) ** -0.25
ATTN_SCALE = (QK_NOPE + QK_ROPE) ** -0.5
NEG_INF = -1e30

kernel_name = "yoco_shortconv_mla_decoder_step"


def _layer_norm(x, g, b):
    xf = x.astype(jnp.float32)
    mu = xf.mean(-1, keepdims=True)
    var = jnp.square(xf - mu).mean(-1, keepdims=True)
    y = (xf - mu) * lax.rsqrt(var + LN_EPS) * g.astype(jnp.float32) + b.astype(jnp.float32)
    return y.astype(x.dtype)


def _rms_norm(x, g):
    xf = x.astype(jnp.float32)
    y = xf * lax.rsqrt(jnp.mean(jnp.square(xf), -1, keepdims=True) + RMS_EPS) * g.astype(jnp.float32)
    return y.astype(x.dtype)


def _rope(x, pos):
    half = QK_ROPE // 2
    inv = 1.0 / (ROPE_THETA ** (jnp.arange(half, dtype=jnp.float32) * (2.0 / QK_ROPE)))
    ang = pos.astype(jnp.float32)[:, None] * inv[None, :]
    shape = (ang.shape[0],) + (1,) * (x.ndim - 3) + (half,)
    cos = jnp.cos(ang).reshape(shape)
    sin = jnp.sin(ang).reshape(shape)
    xf = x.astype(jnp.float32)
    x1, x2 = xf[..., :half], xf[..., half:]
    return jnp.concatenate([x1 * cos - x2 * sin, x1 * sin + x2 * cos], axis=-1).astype(x.dtype)


def _short_conv_mixer(x, conv_state, w_in, w_conv, b_conv, w_out):
    T = x.shape[1]
    g_b, g_c, h = jnp.split(x @ w_in, 3, axis=-1)
    u = g_c * h
    u_ext = jnp.concatenate([conv_state.astype(u.dtype), u], axis=1)
    conv = b_conv + w_conv[0] * u_ext[:, 0:T]
    for k in range(1, CONV_W):
        conv = conv + w_conv[k] * u_ext[:, k:k + T]
    y = (g_b * conv) @ w_out
    return y, u_ext[:, -(CONV_W - 1):]


def _mlp(x, w1, w2):
    return jnp.square(jax.nn.relu(x @ w1)) @ w2


def _mla_kv_side(x, pos, w_dkv, g_kv):
    raw = x @ w_dkv
    ckv = _rms_norm(raw[..., :KV_RANK], g_kv)
    krope = _rope(raw[..., KV_RANK:], pos)
    return ckv, krope


def _mla_q_side(x, pos, w_dq, g_q, w_uq, w_uk):
    cq = _rms_norm(x @ w_dq, g_q)
    q = (cq @ w_uq).reshape(x.shape[:2] + (N_HEADS, QK_NOPE + QK_ROPE))
    q_nope, q_rope = q[..., :QK_NOPE], q[..., QK_NOPE:]
    q_lat = jnp.einsum('bthn,chn->bthc', q_nope, w_uk)
    return q_lat, _rope(q_rope, pos)


def _attend(q_lat, q_rope, ckv, krope, q_pos, k_pos):
    s = (jnp.einsum('bqhc,bkc->bhqk', q_lat, ckv, preferred_element_type=jnp.float32)
         + jnp.einsum('bqhr,bkr->bhqk', q_rope, krope, preferred_element_type=jnp.float32)) * ATTN_SCALE
    mask = k_pos[None, :] <= q_pos[:, None]
    p = jax.nn.softmax(jnp.where(mask[None, None], s, NEG_INF), axis=-1)
    return jnp.einsum('bhqk,bkc->bqhc', p.astype(ckv.dtype), ckv)


def _prompt_attention(q_lat, q_rope, ckv, krope, pos):
    B, T = q_lat.shape[0], q_lat.shape[1]
    n_blk = -(-T // Q_BLOCK)
    pad = n_blk * Q_BLOCK - T
    ql = jnp.pad(q_lat, ((0, 0), (0, pad), (0, 0), (0, 0)))
    qr = jnp.pad(q_rope, ((0, 0), (0, pad), (0, 0), (0, 0)))
    ql = ql.reshape(B, n_blk, Q_BLOCK, N_HEADS, KV_RANK).transpose(1, 0, 2, 3, 4)
    qr = qr.reshape(B, n_blk, Q_BLOCK, N_HEADS, QK_ROPE).transpose(1, 0, 2, 3, 4)
    qpos = jnp.arange(n_blk * Q_BLOCK, dtype=jnp.int32).reshape(n_blk, Q_BLOCK)
    out = lax.map(lambda a: _attend(a[0], a[1], ckv, krope, a[2], pos), (ql, qr, qpos))
    out = out.transpose(1, 0, 2, 3, 4).reshape(B, n_blk * Q_BLOCK, N_HEADS, KV_RANK)
    return out[:, :T]


def _sample_attention(q_lat, q_rope, ckv_new, krope_new, cache_ckv, cache_krope, page_table, q_pos):
    past_len = page_table.shape[1] * PAGE_SIZE
    k_pos = jnp.arange(past_len + ckv_new.shape[1], dtype=jnp.int32)

    def one(a):
        pt, ql, qr, cn, kn = a
        past_c = cache_ckv[pt].reshape(past_len, KV_RANK)
        past_r = cache_krope[pt].reshape(past_len, QK_ROPE)
        kc = jnp.concatenate([past_c.astype(cn.dtype), cn], axis=0)
        kr = jnp.concatenate([past_r.astype(kn.dtype), kn], axis=0)
        return _attend(ql[None], qr[None], kc[None], kr[None], q_pos, k_pos)[0]

    return lax.map(one, (page_table, q_lat, q_rope, ckv_new, krope_new))


def _mla_out(o_lat, w_uv, w_o):
    o = jnp.einsum('bthc,chv->bthv', o_lat, w_uv)
    return o.reshape(o.shape[:2] + (N_HEADS * V_DIM,)) @ w_o


def setup_inputs(seed: int = 0) -> dict:
    key = jax.random.key(seed)
    ks = jax.random.split(key, 32)
    f32 = jnp.float32
    n_pages = PAST_LEN // PAGE_SIZE
    n_used = DEC_BATCH * n_pages
    n_pool = n_used + max(1, n_used // 4)

    def nrm(k, shape, scale):
        return jax.random.normal(k, shape, f32) * scale

    page_table = jax.random.permutation(ks[0], n_pool)[:n_used].reshape(DEC_BATCH, n_pages).astype(jnp.int32)
    return {
        "x_prompt": nrm(ks[1], (BATCH, SEQ, D_MODEL), 1.0),
        "x_sample": nrm(ks[2], (DEC_BATCH, DEC_SEQ, D_MODEL), 1.0),
        "state_conv": nrm(ks[3], (N_A_LAYERS, DEC_BATCH, CONV_W - 1, D_MODEL), 1.0),
        "cache_ckv": nrm(ks[4], (n_pool, PAGE_SIZE, KV_RANK), 1.0),
        "cache_krope": nrm(ks[5], (n_pool, PAGE_SIZE, QK_ROPE), 1.0),
        "page_table": page_table,
        "meta_tokens": nrm(ks[6], (N_META, D_MODEL), 1.0),
        "ln_g": 1.0 + nrm(ks[7], (DEPTH, 2, D_MODEL), 0.01),
        "ln_b": nrm(ks[8], (DEPTH, 2, D_MODEL), 0.01),
        "a_in_w": nrm(ks[9], (N_A_LAYERS, D_MODEL, 3 * D_MODEL), D_MODEL ** -0.5),
        "a_conv_w": nrm(ks[10], (N_A_LAYERS, CONV_W, D_MODEL), CONV_W ** -0.5),
        "a_conv_b": nrm(ks[11], (N_A_LAYERS, D_MODEL), 0.01),
        "a_out_w": nrm(ks[12], (N_A_LAYERS, D_MODEL, D_MODEL), D_MODEL ** -0.5 * DN_BETA),
        "mlp_w1": nrm(ks[13], (DEPTH, D_MODEL, D_FF), D_MODEL ** -0.5 * DN_BETA),
        "mlp_w2": nrm(ks[14], (DEPTH, D_FF, D_MODEL), D_FF ** -0.5 * DN_BETA),
        "b_dq_w": nrm(ks[15], (N_B_LAYERS, D_MODEL, Q_RANK), D_MODEL ** -0.5),
        "b_q_norm_g": 1.0 + nrm(ks[16], (N_B_LAYERS, Q_RANK), 0.01),
        "b_uq_w": nrm(ks[17], (N_B_LAYERS, Q_RANK, N_HEADS * (QK_NOPE + QK_ROPE)), Q_RANK ** -0.5),
        "b_o_w": nrm(ks[18], (N_B_LAYERS, N_HEADS * V_DIM, D_MODEL), (N_HEADS * V_DIM) ** -0.5 * DN_BETA),
        "kv_down_w": nrm(ks[19], (D_MODEL, KV_RANK + QK_ROPE), D_MODEL ** -0.5),
        "kv_norm_g": 1.0 + nrm(ks[20], (KV_RANK,), 0.01),
        "kv_uk_w": nrm(ks[21], (KV_RANK, N_HEADS, QK_NOPE), KV_RANK ** -0.5),
        "kv_uv_w": nrm(ks[22], (KV_RANK, N_HEADS, V_DIM), KV_RANK ** -0.5 * DN_BETA),
    }


def reference(x_prompt, x_sample, state_conv, cache_ckv, cache_krope, page_table, meta_tokens,
              ln_g, ln_b, a_in_w, a_conv_w, a_conv_b, a_out_w, mlp_w1, mlp_w2,
              b_dq_w, b_q_norm_g, b_uq_w, b_o_w, kv_down_w, kv_norm_g, kv_uk_w, kv_uv_w):
    B = x_prompt.shape[0]
    meta = jnp.broadcast_to(meta_tokens.astype(x_prompt.dtype)[None], (B, N_META, D_MODEL))
    xp = jnp.concatenate([meta, x_prompt], axis=1)
    xs = x_sample
    T = xp.shape[1]
    pos_p = jnp.arange(T, dtype=jnp.int32)
    pos_s = PAST_LEN + jnp.arange(xs.shape[1], dtype=jnp.int32)
    zero_conv = jnp.zeros((B, CONV_W - 1, D_MODEL), xp.dtype)

    conv_p, conv_s = [], []
    for i in range(DEPTH):
        if i < N_A_LAYERS:
            mp, cp = _short_conv_mixer(xp, zero_conv, a_in_w[i], a_conv_w[i], a_conv_b[i], a_out_w[i])
            ms, cs = _short_conv_mixer(xs, state_conv[i], a_in_w[i], a_conv_w[i], a_conv_b[i], a_out_w[i])
            conv_p.append(cp)
            conv_s.append(cs)
        else:
            j = i - N_A_LAYERS
            if j == 0:
                ckv_p, kr_p = _mla_kv_side(xp, pos_p, kv_down_w, kv_norm_g)
                ckv_s, kr_s = _mla_kv_side(xs, pos_s, kv_down_w, kv_norm_g)
            ql_p, qr_p = _mla_q_side(xp, pos_p, b_dq_w[j], b_q_norm_g[j], b_uq_w[j], kv_uk_w)
            ql_s, qr_s = _mla_q_side(xs, pos_s, b_dq_w[j], b_q_norm_g[j], b_uq_w[j], kv_uk_w)
            op = _prompt_attention(ql_p, qr_p, ckv_p, kr_p, pos_p)
            os_ = _sample_attention(ql_s, qr_s, ckv_s, kr_s, cache_ckv, cache_krope, page_table, pos_s)
            mp = _mla_out(op, kv_uv_w, b_o_w[j])
            ms = _mla_out(os_, kv_uv_w, b_o_w[j])
        xp = _layer_norm(DN_ALPHA * xp + mp, ln_g[i, 0], ln_b[i, 0])
        xs = _layer_norm(DN_ALPHA * xs + ms, ln_g[i, 0], ln_b[i, 0])
        xp = _layer_norm(DN_ALPHA * xp + _mlp(xp, mlp_w1[i], mlp_w2[i]), ln_g[i, 1], ln_b[i, 1])
        xs = _layer_norm(DN_ALPHA * xs + _mlp(xs, mlp_w1[i], mlp_w2[i]), ln_g[i, 1], ln_b[i, 1])

    y_prompt = xp[:, N_META:]
    y_sample = xs
    state_conv_prompt = jnp.stack(conv_p, axis=0)
    state_conv_sample = jnp.stack(conv_s, axis=0)
    return (y_prompt, y_sample, state_conv_prompt, state_conv_sample, ckv_p, kr_p, ckv_s, kr_s)
```

```python
import functools

import jax
import jax.numpy as jnp
from jax import lax
from jax.experimental import pallas as pl
from jax.experimental.pallas import tpu as pltpu

F32 = jnp.float32
BF16 = jnp.bfloat16

LN_EPS = 1e-5
RMS_EPS = 1e-6
ROPE_THETA = 10000.0
NEG_INF = -1e30

LANES = 128
SUBLANES = 8
CONV_W = 3
VMEM_LIMIT = 56 * 1024 * 1024


def _params(*sem):
    return pltpu.CompilerParams(dimension_semantics=sem, vmem_limit_bytes=VMEM_LIMIT)


def _const(shape):
    zeros = (0,) * len(shape)
    return pl.BlockSpec(shape, lambda *_: zeros, pipeline_mode=pl.Buffered(1))


def _dot(a, b):
    return jnp.dot(a, b, preferred_element_type=F32)


def _dot_nt(a, b):
    return lax.dot_general(a, b, (((1,), (1,)), ((), ())), preferred_element_type=F32)


def _layer_norm(z, g, b):
    mu = jnp.mean(z, axis=-1, keepdims=True)
    zc = z - mu
    var = jnp.mean(jnp.square(zc), axis=-1, keepdims=True)
    return zc * lax.rsqrt(var + LN_EPS) * g + b


def _rms_norm(x, g):
    return x * lax.rsqrt(jnp.mean(jnp.square(x), axis=-1, keepdims=True) + RMS_EPS) * g


def _gated_conv_out(x, u, u1, u2, g_b, wconv_ref, bconv_ref, wout_ref, g_ref, b_ref, alpha):
    wc = wconv_ref[...]
    conv = bconv_ref[...] + wc[0:1] * u2
    conv = conv + wc[1:2] * u1
    conv = conv + wc[2:3] * u
    y = _dot((g_b * conv).astype(BF16), wout_ref[...])
    return _layer_norm(alpha * x + y, g_ref[...], b_ref[...])


def _mixer_prompt_kernel(x_ref, win_ref, wconv_ref, bconv_ref, wout_ref, g_ref, b_ref,
                         o_ref, st_ref, ubuf, *, alpha, tt, st_tile, st_off):
    t = pl.program_id(1)
    d = x_ref.shape[2]

    @pl.when(t == 0)
    def _():
        ubuf[0:SUBLANES, :] = jnp.zeros((SUBLANES, d), F32)

    x = x_ref[0]
    g = _dot(x.astype(BF16), win_ref[...])
    g_b, g_c, h = g[:, :d], g[:, d:2 * d], g[:, 2 * d:]
    u = g_c * h
    ubuf[SUBLANES:SUBLANES + tt, :] = u
    u1 = ubuf[SUBLANES - 1:SUBLANES - 1 + tt, :]
    u2 = ubuf[SUBLANES - 2:SUBLANES - 2 + tt, :]
    o_ref[0] = _gated_conv_out(x, u, u1, u2, g_b, wconv_ref, bconv_ref, wout_ref,
                               g_ref, b_ref, alpha)

    @pl.when(t == st_tile)
    def _():
        st_ref[0] = ubuf[SUBLANES + st_off:2 * SUBLANES + st_off, :]

    ubuf[0:SUBLANES, :] = ubuf[tt:tt + SUBLANES, :]


def _mixer_sample_kernel(x_ref, s1_ref, s2_ref, win_ref, wconv_ref, bconv_ref, wout_ref,
                         g_ref, b_ref, o_ref, u_ref, ubuf, *, alpha, tt, seq):
    d = x_ref.shape[1]
    x = x_ref[...]
    g = _dot(x.astype(BF16), win_ref[...])
    g_b, g_c, h = g[:, :d], g[:, d:2 * d], g[:, 2 * d:]
    u = g_c * h
    u_ref[...] = u
    ubuf[0:SUBLANES, :] = jnp.zeros((SUBLANES, d), F32)
    ubuf[SUBLANES:SUBLANES + tt, :] = u
    r = lax.broadcasted_iota(jnp.int32, (tt, 1), 0) % seq
    u1 = jnp.where(r >= 1, ubuf[SUBLANES - 1:SUBLANES - 1 + tt, :], s1_ref[...])
    u2 = jnp.where(r >= 2, ubuf[SUBLANES - 2:SUBLANES - 2 + tt, :], s2_ref[...])
    o_ref[...] = _gated_conv_out(x, u, u1, u2, g_b, wconv_ref, bconv_ref, wout_ref,
                                 g_ref, b_ref, alpha)


def _mixer_prompt(xp, t_valid, w_in, w_conv, b_conv, w_out, g, b, alpha, tt):
    bsz, tp, d = xp.shape
    st_row = t_valid - SUBLANES
    st_tile, st_off = st_row // tt, st_row % tt
    assert st_off + SUBLANES <= tt
    kern = functools.partial(_mixer_prompt_kernel, alpha=alpha, tt=tt, st_tile=st_tile,
                             st_off=st_off)
    return pl.pallas_call(
        kern,
        grid=(bsz, tp // tt),
        in_specs=[
            pl.BlockSpec((1, tt, d), lambda i, t: (i, t, 0)),
            _const(w_in.shape), _const(w_conv.shape), _const(b_conv.shape), _const(w_out.shape),
            _const(g.shape), _const(b.shape),
        ],
        out_specs=[
            pl.BlockSpec((1, tt, d), lambda i, t: (i, t, 0)),
            pl.BlockSpec((1, SUBLANES, d), lambda i, t: (i, 0, 0)),
        ],
        out_shape=[jax.ShapeDtypeStruct((bsz, tp, d), F32),
                   jax.ShapeDtypeStruct((bsz, SUBLANES, d), F32)],
        scratch_shapes=[pltpu.VMEM((tt + 2 * SUBLANES, d), F32)],
        compiler_params=_params("arbitrary", "arbitrary"),
        name="mixer_prompt",
    )(xp, w_in, w_conv, b_conv, w_out, g, b)


def _mixer_sample(xs, s1, s2, seq, w_in, w_conv, b_conv, w_out, g, b, alpha, tt):
    n, d = xs.shape
    row = pl.BlockSpec((tt, d), lambda t: (t, 0))
    kern = functools.partial(_mixer_sample_kernel, alpha=alpha, tt=tt, seq=seq)
    return pl.pallas_call(
        kern,
        grid=(n // tt,),
        in_specs=[row, row, row,
                  _const(w_in.shape), _const(w_conv.shape), _const(b_conv.shape),
                  _const(w_out.shape), _const(g.shape), _const(b.shape)],
        out_specs=[row, row],
        out_shape=[jax.ShapeDtypeStruct((n, d), F32), jax.ShapeDtypeStruct((n, d), F32)],
        scratch_shapes=[pltpu.VMEM((tt + 2 * SUBLANES, d), F32)],
        compiler_params=_params("arbitrary"),
        name="mixer_sample",
    )(xs, s1, s2, w_in, w_conv, b_conv, w_out, g, b)


def _mlp_kernel(x_ref, w1_ref, w2_ref, g_ref, b_ref, o_ref, *, alpha, chunk):
    x = x_ref[...]
    xb = x.astype(BF16)
    acc = None
    for c0 in range(0, w1_ref.shape[1], chunk):
        h = _dot(xb, w1_ref[:, c0:c0 + chunk])
        h = jnp.square(jnp.maximum(h, 0.0)).astype(BF16)
        part = _dot(h, w2_ref[c0:c0 + chunk, :])
        acc = part if acc is None else acc + part
    o_ref[...] = _layer_norm(alpha * x + acc, g_ref[...], b_ref[...])


def _mlp(x, w1, w2, g, b, alpha, tt):
    n, d = x.shape
    row = pl.BlockSpec((tt, d), lambda t: (t, 0))
    kern = functools.partial(_mlp_kernel, alpha=alpha, chunk=min(w1.shape[1], 1024))
    return pl.pallas_call(
        kern,
        grid=(n // tt,),
        in_specs=[row, _const(w1.shape), _const(w2.shape), _const(g.shape), _const(b.shape)],
        out_specs=row,
        out_shape=jax.ShapeDtypeStruct((n, d), F32),
        compiler_params=_params("arbitrary"),
        name="mlp",
    )(x, w1, w2, g, b)


def _kv_and_q(x, wkv_ref, gkv_ref, wdq_ref, gq_ref, wq1_ref, wq2_ref,
              ck_ref, sk_ref, rank):
    xb = x.astype(BF16)
    raw = _dot(xb, wkv_ref[...])
    ckv = _rms_norm(raw[:, :rank], gkv_ref[...])
    kr = (raw[:, rank:rank + LANES] * ck_ref[...]
          + raw[:, rank + LANES:rank + 2 * LANES] * sk_ref[...])
    cq = _rms_norm(_dot(xb, wdq_ref[...]), gq_ref[...]).astype(BF16)
    qa = _dot(cq, wq1_ref[...])
    qb = _dot(cq, wq2_ref[...])
    return ckv, kr, qa, qb


def _proj_prompt_kernel(x_ref, cq_ref, sq_ref, ck_ref, sk_ref,
                        wkv_ref, gkv_ref, wuk_ref, wdq_ref, gq_ref, wq1_ref, wq2_ref,
                        ckv_o, ckvb_o, kr_o, q_o, k_o, *, rank, dr, heads):
    ckv, kr, qa, qb = _kv_and_q(x_ref[0], wkv_ref, gkv_ref, wdq_ref, gq_ref, wq1_ref, wq2_ref,
                                ck_ref, sk_ref, rank)
    ckv_o[0] = ckv
    ckvb = ckv.astype(BF16)
    ckvb_o[0] = ckvb
    kr_o[0] = kr[:, :dr]
    kn = _dot(ckvb, wuk_ref[...])
    cq_t, sq_t = cq_ref[...], sq_ref[...]
    for h in range(heads):
        sl = slice(h * LANES, (h + 1) * LANES)
        k_o[0, h] = (kn[:, sl] + kr).astype(BF16)
        q_o[0, h] = (qa[:, sl] * cq_t + qb[:, sl] * sq_t).astype(BF16)


def _proj_sample_kernel(x_ref, cq_ref, sq_ref, ck_ref, sk_ref,
                        wkv_ref, gkv_ref, wukt_ref, wdq_ref, gq_ref, wq1_ref, wq2_ref,
                        ckv_o, kr_o, q_o, ql_o, *, rank, dr, heads):
    ckv, kr, qa, qb = _kv_and_q(x_ref[...], wkv_ref, gkv_ref, wdq_ref, gq_ref, wq1_ref,
                                wq2_ref, ck_ref, sk_ref, rank)
    ckv_o[...] = ckv
    kr_o[...] = kr[:, :dr]
    cq_t, sq_t = cq_ref[...], sq_ref[...]
    for h in range(heads):
        sl = slice(h * LANES, (h + 1) * LANES)
        qh = (qa[:, sl] * cq_t + qb[:, sl] * sq_t).astype(BF16)
        q_o[:, sl] = qh
        ql_o[:, h * rank:(h + 1) * rank] = _dot(qh, wukt_ref[h]).astype(BF16)


def _proj_prompt(x, tabs, wkv, gkv, wuk, wdq, gq, wq1, wq2, rank, dr, heads, tt):
    bsz, tp, d = x.shape
    tab = pl.BlockSpec((tt, LANES), lambda i, t: (t, 0))
    kern = functools.partial(_proj_prompt_kernel, rank=rank, dr=dr, heads=heads)
    weights = (wkv, gkv, wuk, wdq, gq, wq1, wq2)
    return pl.pallas_call(
        kern,
        grid=(bsz, tp // tt),
        in_specs=[pl.BlockSpec((1, tt, d), lambda i, t: (i, t, 0)), tab, tab, tab, tab]
                 + [_const(w.shape) for w in weights],
        out_specs=[
            pl.BlockSpec((1, tt, rank), lambda i, t: (i, t, 0)),
            pl.BlockSpec((1, tt, rank), lambda i, t: (i, t, 0)),
            pl.BlockSpec((1, tt, dr), lambda i, t: (i, t, 0)),
            pl.BlockSpec((1, heads, tt, LANES), lambda i, t: (i, 0, t, 0)),
            pl.BlockSpec((1, heads, tt, LANES), lambda i, t: (i, 0, t, 0)),
        ],
        out_shape=[
            jax.ShapeDtypeStruct((bsz, tp, rank), F32),
            jax.ShapeDtypeStruct((bsz, tp, rank), BF16),
            jax.ShapeDtypeStruct((bsz, tp, dr), F32),
            jax.ShapeDtypeStruct((bsz, heads, tp, LANES), BF16),
            jax.ShapeDtypeStruct((bsz, heads, tp, LANES), BF16),
        ],
        compiler_params=_params("arbitrary", "arbitrary"),
        name="proj_prompt",
    )(x, *tabs, *weights)


def _proj_sample(x, tabs, wkv, gkv, wukt, wdq, gq, wq1, wq2, rank, dr, heads, tt):
    n, d = x.shape
    tab = pl.BlockSpec((tt, LANES), lambda t: (t, 0))
    kern = functools.partial(_proj_sample_kernel, rank=rank, dr=dr, heads=heads)
    weights = (wkv, gkv, wukt, wdq, gq, wq1, wq2)
    return pl.pallas_call(
        kern,
        grid=(n // tt,),
        in_specs=[pl.BlockSpec((tt, d), lambda t: (t, 0)), tab, tab, tab, tab]
                 + [_const(w.shape) for w in weights],
        out_specs=[
            pl.BlockSpec((tt, rank), lambda t: (t, 0)),
            pl.BlockSpec((tt, dr), lambda t: (t, 0)),
            pl.BlockSpec((tt, heads * LANES), lambda t: (t, 0)),
            pl.BlockSpec((tt, heads * rank), lambda t: (t, 0)),
        ],
        out_shape=[
            jax.ShapeDtypeStruct((n, rank), F32),
            jax.ShapeDtypeStruct((n, dr), F32),
            jax.ShapeDtypeStruct((n, heads * LANES), BF16),
            jax.ShapeDtypeStruct((n, heads * rank), BF16),
        ],
        compiler_params=_params("arbitrary"),
        name="proj_sample",
    )(x, *tabs, *weights)


def _online_softmax_update(s, v, m_prev, l_prev, acc_prev):
    m_cur = jnp.max(s, axis=1, keepdims=True)
    m_new = jnp.maximum(m_prev, m_cur)
    corr = jnp.exp(m_prev - m_new)
    p = jnp.exp(s - m_new[:, :1])
    l_new = corr * l_prev + jnp.sum(p, axis=1, keepdims=True)
    acc_new = acc_prev * corr[:, :1] + _dot(p.astype(BF16), v)
    return m_new, l_new, acc_new


def _flash_kernel(q_ref, k_ref, v_ref, wuv_ref, o_ref, m_ref, l_ref, acc_ref, *, heads, dv):
    i = pl.program_id(1)
    j = pl.program_id(2)
    tq = q_ref.shape[2]
    tk = k_ref.shape[2]

    @pl.when(j == 0)
    def _():
        m_ref[...] = jnp.full(m_ref.shape, NEG_INF, F32)
        l_ref[...] = jnp.zeros(l_ref.shape, F32)
        acc_ref[...] = jnp.zeros(acc_ref.shape, F32)

    def step(masked):
        v = v_ref[0]
        if masked:
            keep = (lax.broadcasted_iota(jnp.int32, (tq, tk), 1)
                    <= lax.broadcasted_iota(jnp.int32, (tq, tk), 0))

        def body(h, carry):
            s = _dot_nt(q_ref[0, h], k_ref[0, h])
            if masked:
                s = jnp.where(keep, s, NEG_INF)
            m_new, l_new, acc_new = _online_softmax_update(s, v, m_ref[h], l_ref[h], acc_ref[h])
            m_ref[h] = m_new
            l_ref[h] = l_new
            acc_ref[h] = acc_new
            return carry

        lax.fori_loop(0, heads, body, 0)

    @pl.when(j < i)
    def _():
        step(False)

    @pl.when(j == i)
    def _():
        step(True)
        for h in range(heads):
            o_lat = acc_ref[h] / l_ref[h][:, :1]
            o_ref[0, :, h * dv:(h + 1) * dv] = _dot(o_lat.astype(BF16), wuv_ref[h])


def _flash(q, k, v, wuv, tq):
    bsz, heads, tp, _ = q.shape
    rank = v.shape[2]
    dv = wuv.shape[2]
    n = tp // tq
    kern = functools.partial(_flash_kernel, heads=heads, dv=dv)
    return pl.pallas_call(
        kern,
        grid=(bsz, n, n),
        in_specs=[
            pl.BlockSpec((1, heads, tq, LANES), lambda b, i, j: (b, 0, i, 0)),
            pl.BlockSpec((1, heads, tq, LANES), lambda b, i, j: (b, 0, jnp.minimum(i, j), 0)),
            pl.BlockSpec((1, tq, rank), lambda b, i, j: (b, jnp.minimum(i, j), 0)),
            _const(wuv.shape),
        ],
        out_specs=pl.BlockSpec((1, tq, heads * dv), lambda b, i, j: (b, i, 0)),
        out_shape=jax.ShapeDtypeStruct((bsz, tp, heads * dv), F32),
        scratch_shapes=[
            pltpu.VMEM((heads, tq, LANES), F32),
            pltpu.VMEM((heads, tq, LANES), F32),
            pltpu.VMEM((heads, tq, rank), F32),
        ],
        compiler_params=_params("arbitrary", "arbitrary", "arbitrary"),
        name="flash_prompt",
    )(q, k, v, wuv)


def _decode_kernel(pt_ref, ql_ref, qa_ref, cn_ref, rn_ref, cache_c, cache_r, o_ref,
                   cbuf, rbuf, sem, m_ref, l_ref, acc_ref, *, n_chunks, pages, page, heads, dr):
    s = pl.program_id(0)
    c = pl.program_id(1)
    n_seq = pl.num_programs(0)
    step = s * n_chunks + c
    slot = step % 2
    n_pages = n_chunks * pages

    def copies(seq, chunk, slot_):
        out = []
        for p in range(pages):
            pg = pt_ref[seq * n_pages + chunk * pages + p]
            rows = pl.ds(p * page, page)
            out.append(pltpu.make_async_copy(cache_c.at[pg], cbuf.at[slot_, rows], sem.at[0, slot_]))
            out.append(pltpu.make_async_copy(cache_r.at[pg], rbuf.at[slot_, rows], sem.at[1, slot_]))
        return out

    @pl.when(step == 0)
    def _():
        for cp in copies(0, 0, 0):
            cp.start()

    @pl.when(step + 1 < n_seq * n_chunks)
    def _():
        last = c + 1 == n_chunks
        for cp in copies(jnp.where(last, s + 1, s), jnp.where(last, 0, c + 1), 1 - slot):
            cp.start()

    @pl.when(c == 0)
    def _():
        m_ref[...] = jnp.full(m_ref.shape, NEG_INF, F32)
        l_ref[...] = jnp.zeros(l_ref.shape, F32)
        acc_ref[...] = jnp.zeros(acc_ref.shape, F32)

    for cp in copies(s, c, slot):
        cp.wait()

    ql = ql_ref[0]
    qr = qa_ref[0][:, :dr]

    def update(kc, kr, keep=None):
        sc = _dot_nt(ql, kc) + _dot_nt(qr, kr)
        if keep is not None:
            sc = jnp.where(keep, sc, NEG_INF)
        m_new, l_new, acc_new = _online_softmax_update(sc, kc, m_ref[...], l_ref[...], acc_ref[...])
        m_ref[...] = m_new
        l_ref[...] = l_new
        acc_ref[...] = acc_new

    update(cbuf[slot].astype(BF16), rbuf[slot].astype(BF16))

    @pl.when(c == n_chunks - 1)
    def _():
        rows = ql.shape[0]
        n_new = cn_ref.shape[1]
        tok = lax.broadcasted_iota(jnp.int32, (rows, n_new), 0) // heads
        keep = lax.broadcasted_iota(jnp.int32, (rows, n_new), 1) <= tok
        update(cn_ref[0].astype(BF16), rn_ref[0].astype(BF16), keep)
        o_ref[0] = acc_ref[...] / l_ref[...][:, :1]


def _decode(page_table, q_lat, q_arr, ckv_new, kr_new, cache_ckv, cache_kr, heads, pages):
    n_seq, rows, rank = q_lat.shape
    n_new = ckv_new.shape[1]
    dr = kr_new.shape[2]
    page = cache_ckv.shape[1]
    n_pages = page_table.shape[1]
    assert n_pages % pages == 0
    n_chunks = n_pages // pages
    kern = functools.partial(_decode_kernel, n_chunks=n_chunks, pages=pages, page=page,
                             heads=heads, dr=dr)
    grid_spec = pltpu.PrefetchScalarGridSpec(
        num_scalar_prefetch=1,
        grid=(n_seq, n_chunks),
        in_specs=[
            pl.BlockSpec((1, rows, rank), lambda s, c, pt: (s, 0, 0)),
            pl.BlockSpec((1, rows, LANES), lambda s, c, pt: (s, 0, 0)),
            pl.BlockSpec((1, n_new, rank), lambda s, c, pt: (s, 0, 0)),
            pl.BlockSpec((1, n_new, dr), lambda s, c, pt: (s, 0, 0)),
            pl.BlockSpec(memory_space=pl.ANY),
            pl.BlockSpec(memory_space=pl.ANY),
        ],
        out_specs=pl.BlockSpec((1, rows, rank), lambda s, c, pt: (s, 0, 0)),
        scratch_shapes=[
            pltpu.VMEM((2, pages * page, rank), F32),
            pltpu.VMEM((2, pages * page, dr), F32),
            pltpu.SemaphoreType.DMA((2, 2)),
            pltpu.VMEM((rows, LANES), F32),
            pltpu.VMEM((rows, LANES), F32),
            pltpu.VMEM((rows, rank), F32),
        ],
    )
    return pl.pallas_call(
        kern,
        grid_spec=grid_spec,
        out_shape=jax.ShapeDtypeStruct((n_seq, rows, rank), F32),
        compiler_params=_params("arbitrary", "arbitrary"),
        name="decode_sample",
    )(page_table.reshape(-1), q_lat, q_arr, ckv_new, kr_new, cache_ckv, cache_kr)


def _out_prompt_kernel(o_ref, x_ref, wo_ref, g_ref, b_ref, y_ref, *, alpha):
    y = _dot(o_ref[...].astype(BF16), wo_ref[...])
    y_ref[...] = _layer_norm(alpha * x_ref[...] + y, g_ref[...], b_ref[...])


def _out_sample_kernel(ol_ref, x_ref, wuv_ref, wo_ref, g_ref, b_ref, y_ref, *, alpha, heads):
    rank, dv = wuv_ref.shape[1], wuv_ref.shape[2]
    y = None
    for h in range(heads):
        oh = _dot(ol_ref[:, h * rank:(h + 1) * rank].astype(BF16), wuv_ref[h]).astype(BF16)
        part = _dot(oh, wo_ref[h * dv:(h + 1) * dv, :])
        y = part if y is None else y + part
    y_ref[...] = _layer_norm(alpha * x_ref[...] + y, g_ref[...], b_ref[...])


def _out_prompt(o, x, wo, g, b, alpha, tt):
    n, d = x.shape
    kern = functools.partial(_out_prompt_kernel, alpha=alpha)
    return pl.pallas_call(
        kern,
        grid=(n // tt,),
        in_specs=[pl.BlockSpec((tt, o.shape[1]), lambda t: (t, 0)),
                  pl.BlockSpec((tt, d), lambda t: (t, 0)),
                  _const(wo.shape), _const(g.shape), _const(b.shape)],
        out_specs=pl.BlockSpec((tt, d), lambda t: (t, 0)),
        out_shape=jax.ShapeDtypeStruct((n, d), F32),
        compiler_params=_params("arbitrary"),
        name="out_prompt",
    )(o, x, wo, g, b)


def _out_sample(o_lat, x, wuv, wo, g, b, alpha, tt):
    n, d = x.shape
    heads = wuv.shape[0]
    kern = functools.partial(_out_sample_kernel, alpha=alpha, heads=heads)
    return pl.pallas_call(
        kern,
        grid=(n // tt,),
        in_specs=[pl.BlockSpec((tt, o_lat.shape[1]), lambda t: (t, 0)),
                  pl.BlockSpec((tt, d), lambda t: (t, 0)),
                  _const(wuv.shape), _const(wo.shape), _const(g.shape), _const(b.shape)],
        out_specs=pl.BlockSpec((tt, d), lambda t: (t, 0)),
        out_shape=jax.ShapeDtypeStruct((n, d), F32),
        compiler_params=_params("arbitrary"),
        name="out_sample",
    )(o_lat, x, wuv, wo, g, b)


def _rope_tables(pos, dr, dn, scale):
    half = dr // 2
    inv = 1.0 / (ROPE_THETA ** (jnp.arange(half, dtype=F32) * (2.0 / dr)))
    ang = pos.astype(F32)[:, None] * inv[None, :]
    cos, sin = jnp.cos(ang), jnp.sin(ang)
    n = pos.shape[0]
    ck = jnp.concatenate([cos, cos, jnp.zeros((n, LANES - dr), F32)], axis=1)
    sk = jnp.concatenate([-sin, sin, jnp.zeros((n, LANES - dr), F32)], axis=1)
    nope = jnp.concatenate([jnp.zeros((n, dr), F32), jnp.ones((n, dn), F32),
                            jnp.zeros((n, LANES - dr - dn), F32)], axis=1)
    return scale * (ck + nope), scale * sk, ck, sk


def _swap_halves(w):
    half = w.shape[-1] // 2
    return jnp.concatenate([w[..., half:], w[..., :half]], axis=-1)


def _pick_tile(n, prefer):
    for t in prefer:
        if n % t == 0:
            return t
    return n


def kernel(x_prompt, x_sample, state_conv, cache_ckv, cache_krope, page_table, meta_tokens,
           ln_g, ln_b, a_in_w, a_conv_w, a_conv_b, a_out_w, mlp_w1, mlp_w2,
           b_dq_w, b_q_norm_g, b_uq_w, b_o_w, kv_down_w, kv_norm_g, kv_uk_w, kv_uv_w):
    bsz, seq_p, d = x_prompt.shape
    n_meta = meta_tokens.shape[0]
    t_valid = seq_p + n_meta
    n_seq, seq_s, _ = x_sample.shape
    depth = ln_g.shape[0]
    rank, heads, dn = kv_uk_w.shape
    dv = kv_uv_w.shape[2]
    dr = cache_krope.shape[2]
    page = cache_ckv.shape[1]
    past = page_table.shape[1] * page
    q_rank = b_dq_w.shape[2]
    assert depth == 2 and a_in_w.shape[0] == 1 and b_dq_w.shape[0] == 1
    assert a_conv_w.shape[1] == CONV_W and seq_s >= CONV_W - 1 and seq_s % SUBLANES == 0
    assert t_valid % SUBLANES == 0 and rank % LANES == 0 and dr + dn <= LANES and dr % 2 == 0
    alpha = (2.0 * depth) ** 0.25
    scale = float(dn + dr) ** -0.5

    tt = 640 if t_valid >= 4096 else LANES
    tp = -(-t_valid // tt) * tt
    n_p = bsz * tp
    n_s = n_seq * seq_s
    tt_s = _pick_tile(n_s, (512, 256, 128, 64, 32, 16, 8))

    row = lambda v: v.reshape(1, -1).astype(F32)
    bf = lambda w: w.astype(BF16)

    meta = jnp.broadcast_to(meta_tokens.astype(x_prompt.dtype)[None], (bsz, n_meta, d))
    xp = jnp.concatenate([meta, x_prompt, jnp.zeros((bsz, tp - t_valid, d), x_prompt.dtype)], axis=1)
    xs = x_sample.reshape(n_s, d)

    st = state_conv[0]
    zrow = jnp.zeros((n_seq, seq_s, d), F32)
    s1 = zrow.at[:, 0].set(st[:, 1]).reshape(n_s, d)
    s2 = zrow.at[:, 0].set(st[:, 0]).at[:, 1].set(st[:, 1]).reshape(n_s, d)
    a_w = (bf(a_in_w[0]), a_conv_w[0].astype(F32), row(a_conv_b[0]), bf(a_out_w[0]),
           row(ln_g[0, 0]), row(ln_b[0, 0]))
    xp, st_p = _mixer_prompt(xp, t_valid, *a_w, alpha, tt)
    xs, u_s = _mixer_sample(xs, s1, s2, seq_s, *a_w, alpha, tt_s)
    state_conv_prompt = st_p[:, SUBLANES - (CONV_W - 1):][None]
    state_conv_sample = u_s.reshape(n_seq, seq_s, d)[:, seq_s - (CONV_W - 1):][None]

    mlp0 = (bf(mlp_w1[0]), bf(mlp_w2[0]), row(ln_g[0, 1]), row(ln_b[0, 1]))
    xp = _mlp(xp.reshape(n_p, d), *mlp0, alpha, tt)
    xs = _mlp(xs, *mlp0, alpha, tt_s)

    kv_c, kv_r = kv_down_w[:, :rank], kv_down_w[:, rank:]
    zpad = jnp.zeros((d, LANES - dr), F32)
    wkv = bf(jnp.concatenate([kv_c, kv_r, zpad, _swap_halves(kv_r), zpad], axis=1))
    wuk = jnp.concatenate([jnp.zeros((rank, heads, dr), F32), kv_uk_w,
                           jnp.zeros((rank, heads, LANES - dr - dn), F32)], axis=-1)
    wuk_flat = bf(wuk.reshape(rank, heads * LANES))
    wukt = bf(wuk.transpose(1, 2, 0))
    wuq = b_uq_w[0].reshape(q_rank, heads, dn + dr)
    uq_n, uq_r = wuq[..., :dn], wuq[..., dn:]
    wq1 = bf(jnp.concatenate([uq_r, uq_n, jnp.zeros((q_rank, heads, LANES - dr - dn), F32)],
                             axis=-1).reshape(q_rank, heads * LANES))
    wq2 = bf(jnp.concatenate([_swap_halves(uq_r), jnp.zeros((q_rank, heads, LANES - dr), F32)],
                             axis=-1).reshape(q_rank, heads * LANES))
    wuv = bf(kv_uv_w.transpose(1, 0, 2))
    wo = bf(b_o_w[0])
    gkv, gq, wdq = row(kv_norm_g), row(b_q_norm_g[0]), bf(b_dq_w[0])

    tabs_p = _rope_tables(jnp.arange(tp, dtype=jnp.int32), dr, dn, scale)
    pos_s = past + jnp.arange(seq_s, dtype=jnp.int32)
    tabs_s = tuple(jnp.tile(t, (n_seq, 1)) for t in _rope_tables(pos_s, dr, dn, scale))

    ckv_p, ckvb_p, kr_p, q_p, k_p = _proj_prompt(
        xp.reshape(bsz, tp, d), tabs_p, wkv, gkv, wuk_flat, wdq, gq, wq1, wq2, rank, dr, heads, tt)
    ckv_s, kr_s, q_s, ql_s = _proj_sample(
        xs, tabs_s, wkv, gkv, wukt, wdq, gq, wq1, wq2, rank, dr, heads, tt_s)

    o_p = _flash(q_p, k_p, ckvb_p, wuv, tt)
    rows = seq_s * heads
    pages = _pick_tile(page_table.shape[1], (16, 8, 4, 2, 1))
    ol_s = _decode(page_table, ql_s.reshape(n_seq, rows, rank), q_s.reshape(n_seq, rows, LANES),
                   ckv_s.reshape(n_seq, seq_s, rank), kr_s.reshape(n_seq, seq_s, dr),
                   cache_ckv, cache_krope, heads, pages)

    ln10 = (row(ln_g[1, 0]), row(ln_b[1, 0]))
    xp = _out_prompt(o_p.reshape(n_p, heads * dv), xp, wo, *ln10, alpha, tt)
    xs = _out_sample(ol_s.reshape(n_s, heads * rank), xs, wuv, wo, *ln10, alpha, tt_s)

    mlp1 = (bf(mlp_w1[1]), bf(mlp_w2[1]), row(ln_g[1, 1]), row(ln_b[1, 1]))
    xp = _mlp(xp, *mlp1, alpha, tt)
    xs = _mlp(xs, *mlp1, alpha, tt_s)

    y_prompt = xp.reshape(bsz, tp, d)[:, n_meta:t_valid]
    y_sample = xs.reshape(n_seq, seq_s, d)
    return (y_prompt, y_sample, state_conv_prompt, state_conv_sample,
            ckv_p[:, :t_valid], kr_p[:, :t_valid],
            ckv_s.reshape(n_seq, seq_s, rank), kr_s.reshape(n_seq, seq_s, dr))
```

```python
import functools

import jax
import jax.numpy as jnp
from jax import lax
from jax.experimental import pallas as pl
from jax.experimental.pallas import tpu as pltpu

F32 = jnp.float32
BF16 = jnp.bfloat16

LN_EPS = 1e-5
RMS_EPS = 1e-6
ROPE_THETA = 10000.0
NEG_INF = -1e30

LANES = 128
SUBLANES = 8
CONV_W = 3
MXU_WIDTH = 256
ATTN_TILE = 3 * MXU_WIDTH
LOG2_E = 1.4426950408889634
VMEM_LIMIT = 56 * 1024 * 1024


def _params(*sem):
    return pltpu.CompilerParams(dimension_semantics=sem, vmem_limit_bytes=VMEM_LIMIT)


def _const(shape):
    zeros = (0,) * len(shape)
    return pl.BlockSpec(shape, lambda *_: zeros, pipeline_mode=pl.Buffered(1))


def _dot(a, b):
    return jnp.dot(a, b, preferred_element_type=F32)


def _dot_nt(a, b):
    return lax.dot_general(a, b, (((1,), (1,)), ((), ())), preferred_element_type=F32)


def _layer_norm(z, g, b):
    mu = jnp.mean(z, axis=-1, keepdims=True)
    zc = z - mu
    var = jnp.mean(jnp.square(zc), axis=-1, keepdims=True)
    return zc * lax.rsqrt(var + LN_EPS) * g + b


def _rms_norm(x, g):
    return x * lax.rsqrt(jnp.mean(jnp.square(x), axis=-1, keepdims=True) + RMS_EPS) * g


def _gated_conv_out(x, u, u1, u2, g_b, wconv_ref, bconv_ref, wout_ref, g_ref, b_ref, alpha):
    wc = wconv_ref[...]
    conv = bconv_ref[...] + wc[0:1] * u2
    conv = conv + wc[1:2] * u1
    conv = conv + wc[2:3] * u
    y = _dot((g_b * conv).astype(BF16), wout_ref[...])
    return _layer_norm(alpha * x + y, g_ref[...], b_ref[...])


def _mixer_prompt_kernel(x_ref, win_ref, wconv_ref, bconv_ref, wout_ref, g_ref, b_ref,
                         o_ref, st_ref, ubuf, *, alpha, tt, st_tile, st_off):
    t = pl.program_id(1)
    d = x_ref.shape[2]

    @pl.when(t == 0)
    def _():
        ubuf[0:SUBLANES, :] = jnp.zeros((SUBLANES, d), F32)

    x = x_ref[0]
    g = _dot(x.astype(BF16), win_ref[...])
    g_b, g_c, h = g[:, :d], g[:, d:2 * d], g[:, 2 * d:]
    u = g_c * h
    ubuf[SUBLANES:SUBLANES + tt, :] = u
    u1 = ubuf[SUBLANES - 1:SUBLANES - 1 + tt, :]
    u2 = ubuf[SUBLANES - 2:SUBLANES - 2 + tt, :]
    o_ref[0] = _gated_conv_out(x, u, u1, u2, g_b, wconv_ref, bconv_ref, wout_ref,
                               g_ref, b_ref, alpha)

    @pl.when(t == st_tile)
    def _():
        st_ref[0] = ubuf[SUBLANES + st_off:2 * SUBLANES + st_off, :]

    ubuf[0:SUBLANES, :] = ubuf[tt:tt + SUBLANES, :]


def _mixer_sample_kernel(x_ref, s1_ref, s2_ref, win_ref, wconv_ref, bconv_ref, wout_ref,
                         g_ref, b_ref, o_ref, u_ref, ubuf, *, alpha, tt, seq):
    d = x_ref.shape[1]
    x = x_ref[...]
    g = _dot(x.astype(BF16), win_ref[...])
    g_b, g_c, h = g[:, :d], g[:, d:2 * d], g[:, 2 * d:]
    u = g_c * h
    u_ref[...] = u
    ubuf[0:SUBLANES, :] = jnp.zeros((SUBLANES, d), F32)
    ubuf[SUBLANES:SUBLANES + tt, :] = u
    r = lax.broadcasted_iota(jnp.int32, (tt, 1), 0) % seq
    u1 = jnp.where(r >= 1, ubuf[SUBLANES - 1:SUBLANES - 1 + tt, :], s1_ref[...])
    u2 = jnp.where(r >= 2, ubuf[SUBLANES - 2:SUBLANES - 2 + tt, :], s2_ref[...])
    o_ref[...] = _gated_conv_out(x, u, u1, u2, g_b, wconv_ref, bconv_ref, wout_ref,
                                 g_ref, b_ref, alpha)


def _mixer_prompt(xp, t_valid, w_in, w_conv, b_conv, w_out, g, b, alpha, tt):
    bsz, tp, d = xp.shape
    st_row = t_valid - SUBLANES
    st_tile, st_off = st_row // tt, st_row % tt
    assert st_off + SUBLANES <= tt
    kern = functools.partial(_mixer_prompt_kernel, alpha=alpha, tt=tt, st_tile=st_tile,
                             st_off=st_off)
    return pl.pallas_call(
        kern,
        grid=(bsz, tp // tt),
        in_specs=[
            pl.BlockSpec((1, tt, d), lambda i, t: (i, t, 0)),
            _const(w_in.shape), _const(w_conv.shape), _const(b_conv.shape), _const(w_out.shape),
            _const(g.shape), _const(b.shape),
        ],
        out_specs=[
            pl.BlockSpec((1, tt, d), lambda i, t: (i, t, 0)),
            pl.BlockSpec((1, SUBLANES, d), lambda i, t: (i, 0, 0)),
        ],
        out_shape=[jax.ShapeDtypeStruct((bsz, tp, d), F32),
                   jax.ShapeDtypeStruct((bsz, SUBLANES, d), F32)],
        scratch_shapes=[pltpu.VMEM((tt + 2 * SUBLANES, d), F32)],
        compiler_params=_params("arbitrary", "arbitrary"),
        name="mixer_prompt",
    )(xp, w_in, w_conv, b_conv, w_out, g, b)


def _mixer_sample(xs, s1, s2, seq, w_in, w_conv, b_conv, w_out, g, b, alpha, tt):
    n, d = xs.shape
    row = pl.BlockSpec((tt, d), lambda t: (t, 0))
    kern = functools.partial(_mixer_sample_kernel, alpha=alpha, tt=tt, seq=seq)
    return pl.pallas_call(
        kern,
        grid=(n // tt,),
        in_specs=[row, row, row,
                  _const(w_in.shape), _const(w_conv.shape), _const(b_conv.shape),
                  _const(w_out.shape), _const(g.shape), _const(b.shape)],
        out_specs=[row, row],
        out_shape=[jax.ShapeDtypeStruct((n, d), F32), jax.ShapeDtypeStruct((n, d), F32)],
        scratch_shapes=[pltpu.VMEM((tt + 2 * SUBLANES, d), F32)],
        compiler_params=_params("arbitrary"),
        name="mixer_sample",
    )(xs, s1, s2, w_in, w_conv, b_conv, w_out, g, b)


def _mlp_kernel(x_ref, w1_ref, w2_ref, g_ref, b_ref, o_ref, *, alpha, chunk):
    x = x_ref[...]
    xb = x.astype(BF16)
    acc = None
    for c0 in range(0, w1_ref.shape[1], chunk):
        h = _dot(xb, w1_ref[:, c0:c0 + chunk])
        h = jnp.square(jnp.maximum(h, 0.0)).astype(BF16)
        part = _dot(h, w2_ref[c0:c0 + chunk, :])
        acc = part if acc is None else acc + part
    o_ref[...] = _layer_norm(alpha * x + acc, g_ref[...], b_ref[...])


def _mlp(x, w1, w2, g, b, alpha, tt):
    n, d = x.shape
    row = pl.BlockSpec((tt, d), lambda t: (t, 0))
    kern = functools.partial(_mlp_kernel, alpha=alpha, chunk=min(w1.shape[1], 1024))
    return pl.pallas_call(
        kern,
        grid=(n // tt,),
        in_specs=[row, _const(w1.shape), _const(w2.shape), _const(g.shape), _const(b.shape)],
        out_specs=row,
        out_shape=jax.ShapeDtypeStruct((n, d), F32),
        compiler_params=_params("arbitrary"),
        name="mlp",
    )(x, w1, w2, g, b)


def _kv_and_q(x, wkv_ref, gkv_ref, wdq_ref, gq_ref, wq1_ref, wq2_ref,
              ck_ref, sk_ref, rank):
    xb = x.astype(BF16)
    raw = _dot(xb, wkv_ref[...])
    ckv = _rms_norm(raw[:, :rank], gkv_ref[...])
    kr = (raw[:, rank:rank + LANES] * ck_ref[...]
          + raw[:, rank + LANES:rank + 2 * LANES] * sk_ref[...])
    cq = _rms_norm(_dot(xb, wdq_ref[...]), gq_ref[...]).astype(BF16)
    qa = _dot(cq, wq1_ref[...])
    qb = _dot(cq, wq2_ref[...])
    return ckv, kr, qa, qb


def _proj_prompt_kernel(x_ref, cq_ref, sq_ref, ck_ref, sk_ref,
                        wkv_ref, gkv_ref, wuk_ref, wdq_ref, gq_ref, wq1_ref, wq2_ref,
                        ckv_o, ckvb_o, kr_o, q_o, k_o, *, rank, dr, heads):
    ckv, kr, qa, qb = _kv_and_q(x_ref[0], wkv_ref, gkv_ref, wdq_ref, gq_ref, wq1_ref, wq2_ref,
                                ck_ref, sk_ref, rank)
    ckv_o[0] = ckv
    ckvb = ckv.astype(BF16)
    ckvb_o[0] = ckvb
    kr_o[0] = kr[:, :dr]
    kn = _dot(ckvb, wuk_ref[...])
    cq_t, sq_t = cq_ref[...], sq_ref[...]
    for h in range(heads):
        sl = slice(h * LANES, (h + 1) * LANES)
        k_o[0, h] = (kn[:, sl] + kr).astype(BF16)
        q_o[0, h] = (qa[:, sl] * cq_t + qb[:, sl] * sq_t).astype(BF16)


def _proj_sample_kernel(x_ref, cq_ref, sq_ref, ck_ref, sk_ref,
                        wkv_ref, gkv_ref, wukt_ref, wdq_ref, gq_ref, wq1_ref, wq2_ref,
                        ckv_o, kr_o, q_o, ql_o, *, rank, dr, heads):
    ckv, kr, qa, qb = _kv_and_q(x_ref[...], wkv_ref, gkv_ref, wdq_ref, gq_ref, wq1_ref,
                                wq2_ref, ck_ref, sk_ref, rank)
    ckv_o[...] = ckv
    kr_o[...] = kr[:, :dr]
    cq_t, sq_t = cq_ref[...], sq_ref[...]
    for h in range(heads):
        sl = slice(h * LANES, (h + 1) * LANES)
        qh = (qa[:, sl] * cq_t + qb[:, sl] * sq_t).astype(BF16)
        q_o[:, sl] = qh
        ql_o[:, h * rank:(h + 1) * rank] = _dot(qh, wukt_ref[h]).astype(BF16)


def _proj_prompt(x, tabs, wkv, gkv, wuk, wdq, gq, wq1, wq2, rank, dr, heads, tt):
    bsz, tp, d = x.shape
    tab = pl.BlockSpec((tt, LANES), lambda i, t: (t, 0))
    kern = functools.partial(_proj_prompt_kernel, rank=rank, dr=dr, heads=heads)
    weights = (wkv, gkv, wuk, wdq, gq, wq1, wq2)
    return pl.pallas_call(
        kern,
        grid=(bsz, tp // tt),
        in_specs=[pl.BlockSpec((1, tt, d), lambda i, t: (i, t, 0)), tab, tab, tab, tab]
                 + [_const(w.shape) for w in weights],
        out_specs=[
            pl.BlockSpec((1, tt, rank), lambda i, t: (i, t, 0)),
            pl.BlockSpec((1, tt, rank), lambda i, t: (i, t, 0)),
            pl.BlockSpec((1, tt, dr), lambda i, t: (i, t, 0)),
            pl.BlockSpec((1, heads, tt, LANES), lambda i, t: (i, 0, t, 0)),
            pl.BlockSpec((1, heads, tt, LANES), lambda i, t: (i, 0, t, 0)),
        ],
        out_shape=[
            jax.ShapeDtypeStruct((bsz, tp, rank), F32),
            jax.ShapeDtypeStruct((bsz, tp, rank), BF16),
            jax.ShapeDtypeStruct((bsz, tp, dr), F32),
            jax.ShapeDtypeStruct((bsz, heads, tp, LANES), BF16),
            jax.ShapeDtypeStruct((bsz, heads, tp, LANES), BF16),
        ],
        compiler_params=_params("arbitrary", "arbitrary"),
        name="proj_prompt",
    )(x, *tabs, *weights)


def _proj_sample(x, tabs, wkv, gkv, wukt, wdq, gq, wq1, wq2, rank, dr, heads, tt):
    n, d = x.shape
    tab = pl.BlockSpec((tt, LANES), lambda t: (t, 0))
    kern = functools.partial(_proj_sample_kernel, rank=rank, dr=dr, heads=heads)
    weights = (wkv, gkv, wukt, wdq, gq, wq1, wq2)
    return pl.pallas_call(
        kern,
        grid=(n // tt,),
        in_specs=[pl.BlockSpec((tt, d), lambda t: (t, 0)), tab, tab, tab, tab]
                 + [_const(w.shape) for w in weights],
        out_specs=[
            pl.BlockSpec((tt, rank), lambda t: (t, 0)),
            pl.BlockSpec((tt, dr), lambda t: (t, 0)),
            pl.BlockSpec((tt, heads * LANES), lambda t: (t, 0)),
            pl.BlockSpec((tt, heads * rank), lambda t: (t, 0)),
        ],
        out_shape=[
            jax.ShapeDtypeStruct((n, rank), F32),
            jax.ShapeDtypeStruct((n, dr), F32),
            jax.ShapeDtypeStruct((n, heads * LANES), BF16),
            jax.ShapeDtypeStruct((n, heads * rank), BF16),
        ],
        compiler_params=_params("arbitrary"),
        name="proj_sample",
    )(x, *tabs, *weights)


def _lane_tile(x, width):
    if width <= LANES:
        return x[:, :width]
    return jnp.concatenate([x] * (width // LANES), axis=1)


def _online_softmax_update(s, v, m_prev, l_prev, acc_prev):
    m_cur = jnp.max(s, axis=1, keepdims=True)
    m_new = jnp.maximum(m_prev, m_cur)
    corr = jnp.exp2(m_prev - m_new)
    p = jnp.exp2(s - _lane_tile(m_new, s.shape[1]))
    l_new = corr * l_prev + jnp.sum(p, axis=1, keepdims=True)
    acc_new = acc_prev * _lane_tile(corr, acc_prev.shape[1]) + _dot(p.astype(BF16), v)
    return m_new, l_new, acc_new


def _flash_kernel(q_ref, k_ref, v_ref, wuv_ref, o_ref, m_ref, l_ref, acc_ref, s_a, s_b,
                  *, heads, dv):
    i = pl.program_id(1)
    j = pl.program_id(2)
    tq = q_ref.shape[2]
    tk = k_ref.shape[2]

    @pl.when(j == 0)
    def _():
        m_ref[...] = jnp.full(m_ref.shape, NEG_INF, F32)
        l_ref[...] = jnp.zeros(l_ref.shape, F32)
        acc_ref[...] = jnp.zeros(acc_ref.shape, F32)

    def step(masked):
        v = v_ref[0]
        if masked:
            keep = (lax.broadcasted_iota(jnp.int32, (tq, tk), 1)
                    <= lax.broadcasted_iota(jnp.int32, (tq, tk), 0))

        def scores(h, dst):
            dst[...] = _dot_nt(q_ref[0, h], k_ref[0, h])

        def attend(h, src):
            s = src[...]
            if masked:
                s = jnp.where(keep, s, NEG_INF)
            m_new, l_new, acc_new = _online_softmax_update(s, v, m_ref[h], l_ref[h], acc_ref[h])
            m_ref[h] = m_new
            l_ref[h] = l_new
            acc_ref[h] = acc_new

        def pair(h, has_next):
            scores(h + 1, s_b)
            attend(h, s_a)
            if has_next:
                scores(h + 2, s_a)
            attend(h + 1, s_b)

        def body(t, carry):
            pair(2 * t, True)
            return carry

        scores(0, s_a)
        lax.fori_loop(0, heads // 2 - 1, body, 0)
        pair(heads - 2, False)

    @pl.when(j < i)
    def _():
        step(False)

    @pl.when(j == i)
    def _():
        step(True)
        for h in range(heads):
            o_lat = acc_ref[h] / _lane_tile(l_ref[h], acc_ref.shape[2])
            o_ref[0, :, h * dv:(h + 1) * dv] = _dot(o_lat.astype(BF16), wuv_ref[h])


def _flash(q, k, v, wuv, tq):
    bsz, heads, tp, _ = q.shape
    rank = v.shape[2]
    dv = wuv.shape[2]
    n = tp // tq
    assert heads % 2 == 0
    kern = functools.partial(_flash_kernel, heads=heads, dv=dv)
    return pl.pallas_call(
        kern,
        grid=(bsz, n, n),
        in_specs=[
            pl.BlockSpec((1, heads, tq, LANES), lambda b, i, j: (b, 0, i, 0)),
            pl.BlockSpec((1, heads, tq, LANES), lambda b, i, j: (b, 0, jnp.minimum(i, j), 0)),
            pl.BlockSpec((1, tq, rank), lambda b, i, j: (b, jnp.minimum(i, j), 0)),
            _const(wuv.shape),
        ],
        out_specs=pl.BlockSpec((1, tq, heads * dv), lambda b, i, j: (b, i, 0)),
        out_shape=jax.ShapeDtypeStruct((bsz, tp, heads * dv), F32),
        scratch_shapes=[
            pltpu.VMEM((heads, tq, LANES), F32),
            pltpu.VMEM((heads, tq, LANES), F32),
            pltpu.VMEM((heads, tq, rank), F32),
            pltpu.VMEM((tq, tq), F32),
            pltpu.VMEM((tq, tq), F32),
        ],
        compiler_params=_params("arbitrary", "arbitrary", "arbitrary"),
        name="flash_prompt",
    )(q, k, v, wuv)


def _decode_kernel(pt_ref, ql_ref, qa_ref, cn_ref, rn_ref, cache_c, cache_r, o_ref,
                   cbuf, rbuf, sem, m_ref, l_ref, acc_ref, *, n_chunks, pages, page, heads, dr):
    s = pl.program_id(0)
    c = pl.program_id(1)
    n_seq = pl.num_programs(0)
    step = s * n_chunks + c
    slot = step % 2
    n_pages = n_chunks * pages

    def copies(seq, chunk, slot_):
        out = []
        for p in range(pages):
            pg = pt_ref[seq * n_pages + chunk * pages + p]
            rows = pl.ds(p * page, page)
            out.append(pltpu.make_async_copy(cache_c.at[pg], cbuf.at[slot_, rows], sem.at[0, slot_]))
            out.append(pltpu.make_async_copy(cache_r.at[pg], rbuf.at[slot_, :, rows], sem.at[1, slot_]))
        return out

    @pl.when(step == 0)
    def _():
        for cp in copies(0, 0, 0):
            cp.start()

    @pl.when(step + 1 < n_seq * n_chunks)
    def _():
        last = c + 1 == n_chunks
        for cp in copies(jnp.where(last, s + 1, s), jnp.where(last, 0, c + 1), 1 - slot):
            cp.start()

    @pl.when(c == 0)
    def _():
        m_ref[...] = jnp.full(m_ref.shape, NEG_INF, F32)
        l_ref[...] = jnp.zeros(l_ref.shape, F32)
        acc_ref[...] = jnp.zeros(acc_ref.shape, F32)

    for cp in copies(s, c, slot):
        cp.wait()

    ql = ql_ref[0]
    qr = qa_ref[0][:, :dr]

    def update(a, kc, krt, keep=None):
        sc = _dot_nt(ql, kc) + _dot(qr, krt)
        if keep is not None:
            sc = jnp.where(keep, sc, NEG_INF)
        m_new, l_new, acc_new = _online_softmax_update(sc, kc, m_ref[a], l_ref[a], acc_ref[a])
        m_ref[a] = m_new
        l_ref[a] = l_new
        acc_ref[a] = acc_new

    parts = m_ref.shape[0]
    width = pages * page // parts
    for a in range(parts):
        keys = pl.ds(a * width, width)
        update(a, cbuf[slot, keys, :].astype(BF16), rbuf[slot, :, keys].astype(BF16))

    @pl.when(c == n_chunks - 1)
    def _():
        rows = ql.shape[0]
        n_new = cn_ref.shape[1]
        tok = lax.broadcasted_iota(jnp.int32, (rows, n_new), 0) // heads
        keep = lax.broadcasted_iota(jnp.int32, (rows, n_new), 1) <= tok
        update(0, cn_ref[0].astype(BF16), rn_ref[0].astype(BF16), keep)
        m_all = m_ref[0]
        for a in range(1, parts):
            m_all = jnp.maximum(m_all, m_ref[a])
        l_all = acc_all = None
        for a in range(parts):
            w = jnp.exp2(m_ref[a] - m_all)
            l_a = w * l_ref[a]
            acc_a = _lane_tile(w, acc_ref.shape[2]) * acc_ref[a]
            l_all = l_a if l_all is None else l_all + l_a
            acc_all = acc_a if acc_all is None else acc_all + acc_a
        o_ref[0] = acc_all / _lane_tile(l_all, acc_ref.shape[2])


def _decode(page_table, q_lat, q_arr, ckv_new, krt_new, cache_ckv, cache_krt, heads, pages):
    n_seq, rows, rank = q_lat.shape
    n_new = ckv_new.shape[1]
    dr = krt_new.shape[1]
    page = cache_ckv.shape[1]
    n_pages = page_table.shape[1]
    assert n_pages % pages == 0
    n_chunks = n_pages // pages
    parts = 2 if pages % 2 == 0 else 1
    kern = functools.partial(_decode_kernel, n_chunks=n_chunks, pages=pages, page=page,
                             heads=heads, dr=dr)
    grid_spec = pltpu.PrefetchScalarGridSpec(
        num_scalar_prefetch=1,
        grid=(n_seq, n_chunks),
        in_specs=[
            pl.BlockSpec((1, rows, rank), lambda s, c, pt: (s, 0, 0)),
            pl.BlockSpec((1, rows, LANES), lambda s, c, pt: (s, 0, 0)),
            pl.BlockSpec((1, n_new, rank), lambda s, c, pt: (s, 0, 0)),
            pl.BlockSpec((1, dr, n_new), lambda s, c, pt: (s, 0, 0)),
            pl.BlockSpec(memory_space=pl.ANY),
            pl.BlockSpec(memory_space=pl.ANY),
        ],
        out_specs=pl.BlockSpec((1, rows, rank), lambda s, c, pt: (s, 0, 0)),
        scratch_shapes=[
            pltpu.VMEM((2, pages * page, rank), F32),
            pltpu.VMEM((2, dr, pages * page), F32),
            pltpu.SemaphoreType.DMA((2, 2)),
            pltpu.VMEM((parts, rows, LANES), F32),
            pltpu.VMEM((parts, rows, LANES), F32),
            pltpu.VMEM((parts, rows, rank), F32),
        ],
    )
    return pl.pallas_call(
        kern,
        grid_spec=grid_spec,
        out_shape=jax.ShapeDtypeStruct((n_seq, rows, rank), F32),
        compiler_params=_params("arbitrary", "arbitrary"),
        name="decode_sample",
    )(page_table.reshape(-1), q_lat, q_arr, ckv_new, krt_new, cache_ckv, cache_krt)


def _out_prompt_kernel(o_ref, x_ref, wo_ref, g_ref, b_ref, y_ref, *, alpha):
    y = _dot(o_ref[...].astype(BF16), wo_ref[...])
    y_ref[...] = _layer_norm(alpha * x_ref[...] + y, g_ref[...], b_ref[...])


def _out_sample_kernel(ol_ref, x_ref, wuv_ref, wo_ref, g_ref, b_ref, y_ref, *, alpha, heads):
    rank, dv = wuv_ref.shape[1], wuv_ref.shape[2]
    y = None
    for h in range(heads):
        oh = _dot(ol_ref[:, h * rank:(h + 1) * rank].astype(BF16), wuv_ref[h]).astype(BF16)
        part = _dot(oh, wo_ref[h * dv:(h + 1) * dv, :])
        y = part if y is None else y + part
    y_ref[...] = _layer_norm(alpha * x_ref[...] + y, g_ref[...], b_ref[...])


def _out_prompt(o, x, wo, g, b, alpha, tt):
    n, d = x.shape
    kern = functools.partial(_out_prompt_kernel, alpha=alpha)
    return pl.pallas_call(
        kern,
        grid=(n // tt,),
        in_specs=[pl.BlockSpec((tt, o.shape[1]), lambda t: (t, 0)),
                  pl.BlockSpec((tt, d), lambda t: (t, 0)),
                  _const(wo.shape), _const(g.shape), _const(b.shape)],
        out_specs=pl.BlockSpec((tt, d), lambda t: (t, 0)),
        out_shape=jax.ShapeDtypeStruct((n, d), F32),
        compiler_params=_params("arbitrary"),
        name="out_prompt",
    )(o, x, wo, g, b)


def _out_sample(o_lat, x, wuv, wo, g, b, alpha, tt):
    n, d = x.shape
    heads = wuv.shape[0]
    kern = functools.partial(_out_sample_kernel, alpha=alpha, heads=heads)
    return pl.pallas_call(
        kern,
        grid=(n // tt,),
        in_specs=[pl.BlockSpec((tt, o_lat.shape[1]), lambda t: (t, 0)),
                  pl.BlockSpec((tt, d), lambda t: (t, 0)),
                  _const(wuv.shape), _const(wo.shape), _const(g.shape), _const(b.shape)],
        out_specs=pl.BlockSpec((tt, d), lambda t: (t, 0)),
        out_shape=jax.ShapeDtypeStruct((n, d), F32),
        compiler_params=_params("arbitrary"),
        name="out_sample",
    )(o_lat, x, wuv, wo, g, b)


def _rope_tables(pos, dr, dn, scale):
    half = dr // 2
    inv = 1.0 / (ROPE_THETA ** (jnp.arange(half, dtype=F32) * (2.0 / dr)))
    ang = pos.astype(F32)[:, None] * inv[None, :]
    cos, sin = jnp.cos(ang), jnp.sin(ang)
    n = pos.shape[0]
    ck = jnp.concatenate([cos, cos, jnp.zeros((n, LANES - dr), F32)], axis=1)
    sk = jnp.concatenate([-sin, sin, jnp.zeros((n, LANES - dr), F32)], axis=1)
    nope = jnp.concatenate([jnp.zeros((n, dr), F32), jnp.ones((n, dn), F32),
                            jnp.zeros((n, LANES - dr - dn), F32)], axis=1)
    return scale * (ck + nope), scale * sk, ck, sk


def _swap_halves(w):
    half = w.shape[-1] // 2
    return jnp.concatenate([w[..., half:], w[..., :half]], axis=-1)


def _pick_tile(n, prefer):
    for t in prefer:
        if n % t == 0:
            return t
    return n


def kernel(x_prompt, x_sample, state_conv, cache_ckv, cache_krope, page_table, meta_tokens,
           ln_g, ln_b, a_in_w, a_conv_w, a_conv_b, a_out_w, mlp_w1, mlp_w2,
           b_dq_w, b_q_norm_g, b_uq_w, b_o_w, kv_down_w, kv_norm_g, kv_uk_w, kv_uv_w):
    bsz, seq_p, d = x_prompt.shape
    n_meta = meta_tokens.shape[0]
    t_valid = seq_p + n_meta
    n_seq, seq_s, _ = x_sample.shape
    depth = ln_g.shape[0]
    rank, heads, dn = kv_uk_w.shape
    dv = kv_uv_w.shape[2]
    dr = cache_krope.shape[2]
    page = cache_ckv.shape[1]
    past = page_table.shape[1] * page
    q_rank = b_dq_w.shape[2]
    assert depth == 2 and a_in_w.shape[0] == 1 and b_dq_w.shape[0] == 1
    assert a_conv_w.shape[1] == CONV_W and seq_s >= CONV_W - 1 and seq_s % SUBLANES == 0
    assert t_valid % SUBLANES == 0 and rank % LANES == 0 and dr + dn <= LANES and dr % 2 == 0
    alpha = (2.0 * depth) ** 0.25
    scale = float(dn + dr) ** -0.5 * LOG2_E

    ta, tt = (ATTN_TILE, ATTN_TILE // 2) if t_valid >= 4 * ATTN_TILE else (LANES, LANES)
    tp = -(-t_valid // ta) * ta
    n_p = bsz * tp
    n_s = n_seq * seq_s
    tt_s = _pick_tile(n_s, (512, 256, 128, 64, 32, 16, 8))

    row = lambda v: v.reshape(1, -1).astype(F32)
    bf = lambda w: w.astype(BF16)

    meta = jnp.broadcast_to(meta_tokens.astype(x_prompt.dtype)[None], (bsz, n_meta, d))
    xp = jnp.concatenate([meta, x_prompt, jnp.zeros((bsz, tp - t_valid, d), x_prompt.dtype)], axis=1)
    xs = x_sample.reshape(n_s, d)

    st = state_conv[0]
    zrow = jnp.zeros((n_seq, seq_s, d), F32)
    s1 = zrow.at[:, 0].set(st[:, 1]).reshape(n_s, d)
    s2 = zrow.at[:, 0].set(st[:, 0]).at[:, 1].set(st[:, 1]).reshape(n_s, d)
    a_w = (bf(a_in_w[0]), a_conv_w[0].astype(F32), row(a_conv_b[0]), bf(a_out_w[0]),
           row(ln_g[0, 0]), row(ln_b[0, 0]))
    xp, st_p = _mixer_prompt(xp, t_valid, *a_w, alpha, tt)
    xs, u_s = _mixer_sample(xs, s1, s2, seq_s, *a_w, alpha, tt_s)
    state_conv_prompt = st_p[:, SUBLANES - (CONV_W - 1):][None]
    state_conv_sample = u_s.reshape(n_seq, seq_s, d)[:, seq_s - (CONV_W - 1):][None]

    mlp0 = (bf(mlp_w1[0]), bf(mlp_w2[0]), row(ln_g[0, 1]), row(ln_b[0, 1]))
    xp = _mlp(xp.reshape(n_p, d), *mlp0, alpha, ta)
    xs = _mlp(xs, *mlp0, alpha, tt_s)

    kv_c, kv_r = kv_down_w[:, :rank], kv_down_w[:, rank:]
    zpad = jnp.zeros((d, LANES - dr), F32)
    wkv = bf(jnp.concatenate([kv_c, kv_r, zpad, _swap_halves(kv_r), zpad], axis=1))
    wuk = jnp.concatenate([jnp.zeros((rank, heads, dr), F32), kv_uk_w,
                           jnp.zeros((rank, heads, LANES - dr - dn), F32)], axis=-1)
    wuk_flat = bf(wuk.reshape(rank, heads * LANES))
    wukt = bf(wuk.transpose(1, 2, 0))
    wuq = b_uq_w[0].reshape(q_rank, heads, dn + dr)
    uq_n, uq_r = wuq[..., :dn], wuq[..., dn:]
    wq1 = bf(jnp.concatenate([uq_r, uq_n, jnp.zeros((q_rank, heads, LANES - dr - dn), F32)],
                             axis=-1).reshape(q_rank, heads * LANES))
    wq2 = bf(jnp.concatenate([_swap_halves(uq_r), jnp.zeros((q_rank, heads, LANES - dr), F32)],
                             axis=-1).reshape(q_rank, heads * LANES))
    wuv = bf(kv_uv_w.transpose(1, 0, 2))
    wo = bf(b_o_w[0])
    gkv, gq, wdq = row(kv_norm_g), row(b_q_norm_g[0]), bf(b_dq_w[0])

    tabs_p = _rope_tables(jnp.arange(tp, dtype=jnp.int32), dr, dn, scale)
    pos_s = past + jnp.arange(seq_s, dtype=jnp.int32)
    tabs_s = tuple(jnp.tile(t, (n_seq, 1)) for t in _rope_tables(pos_s, dr, dn, scale))

    ckv_p, ckvb_p, kr_p, q_p, k_p = _proj_prompt(
        xp.reshape(bsz, tp, d), tabs_p, wkv, gkv, wuk_flat, wdq, gq, wq1, wq2, rank, dr, heads, tt)
    ckv_s, kr_s, q_s, ql_s = _proj_sample(
        xs, tabs_s, wkv, gkv, wukt, wdq, gq, wq1, wq2, rank, dr, heads, tt_s)

    o_p = _flash(q_p, k_p, ckvb_p, wuv, ta)
    rows = seq_s * heads
    pages = _pick_tile(page_table.shape[1], (16, 8, 4, 2, 1))
    ol_s = _decode(page_table, ql_s.reshape(n_seq, rows, rank), q_s.reshape(n_seq, rows, LANES),
                   ckv_s.reshape(n_seq, seq_s, rank),
                   kr_s.reshape(n_seq, seq_s, dr).transpose(0, 2, 1),
                   cache_ckv, cache_krope.transpose(0, 2, 1), heads, pages)

    ln10 = (row(ln_g[1, 0]), row(ln_b[1, 0]))
    xp = _out_prompt(o_p.reshape(n_p, heads * dv), xp, wo, *ln10, alpha, ta)
    xs = _out_sample(ol_s.reshape(n_s, heads * rank), xs, wuv, wo, *ln10, alpha, tt_s)

    mlp1 = (bf(mlp_w1[1]), bf(mlp_w2[1]), row(ln_g[1, 1]), row(ln_b[1, 1]))
    xp = _mlp(xp, *mlp1, alpha, ta)
    xs = _mlp(xs, *mlp1, alpha, tt_s)

    y_prompt = xp.reshape(bsz, tp, d)[:, n_meta:t_valid]
    y_sample = xs.reshape(n_seq, seq_s, d)
    return (y_prompt, y_sample, state_conv_prompt, state_conv_sample,
            ckv_p[:, :t_valid], kr_p[:, :t_valid],
            ckv_s.reshape(n_seq, seq_s, rank), kr_s.reshape(n_seq, seq_s, dr))
```
